```python
import jax, jax.numpy as jnp
from jax import lax
import numpy as np

D_MODEL = 1024
BATCH = 32
SEQ = 2048
DEPTH = 2
DEC_BATCH = 16
DEC_SEQ = 32
PAST_LEN = 4096

CHUNK = 64
D_MIX = D_MODEL
W_A = D_MIX // 2
W_B = D_MIX - W_A
H_A = 8
DH_A = W_A // H_A
A_CHUNK = 128
H_B = 8
CONV_K = 31
CONV_STATE = CONV_K - 1
D_IN = 3 * W_A + 3 * W_B
ALPHA = (2 * DEPTH) ** 0.25
BETA = (8 * DEPTH) ** -0.25
LN_EPS = 1e-5

kernel_name = "hymba_gmlp_conformer_stream_step"


def layer_norm(x, g, b):
    xf = x.astype(jnp.float32)
    mu = jnp.mean(xf, axis=-1, keepdims=True)
    var = jnp.mean(jnp.square(xf - mu), axis=-1, keepdims=True)
    y = (xf - mu) * lax.rsqrt(var + LN_EPS)
    return (y * g.astype(jnp.float32) + b.astype(jnp.float32)).astype(x.dtype)


def spatial_gating(v, ws, bias):
    bsz, L, _ = v.shape
    lc = min(L, A_CHUNK)
    n = L // lc
    mask = jnp.tril(jnp.ones((lc, lc), dtype=bool))
    ws_m = jnp.where(mask[None], ws[:, :lc, :lc], jnp.zeros((), ws.dtype))
    vr = v.reshape(bsz, n, lc, H_A, DH_A)
    s = jnp.einsum('hqk,bnkhc->bnqhc', ws_m, vr)
    s = s + jnp.transpose(bias[:, :lc])[None, None, :, :, None]
    return s.reshape(bsz, L, W_A)


def depthwise_causal_conv(hist, a, w, b):
    full = jnp.concatenate([hist, a], axis=1)
    out = lax.conv_general_dilated(full, w[:, None, :].astype(full.dtype), window_strides=(1,),
                                   padding='VALID', dimension_numbers=('NWC', 'WIO', 'NWC'),
                                   feature_group_count=W_B)
    return out + b, full[:, -CONV_STATE:]


def layer(x, conv_hist, w_in, a_ln_g, a_ln_b, a_ws, a_bias, b_conv_w, b_conv_b,
          b_ln_g, b_ln_b, w_out, post_ln_g, post_ln_b):
    h = jnp.einsum('bsd,de->bse', x, w_in)
    u, v, ga, bv, bg, gb = jnp.split(
        h, [W_A, 2 * W_A, 3 * W_A, 3 * W_A + W_B, 3 * W_A + 2 * W_B], axis=-1)
    u = jax.nn.gelu(u, approximate=False)
    v = layer_norm(jax.nn.gelu(v, approximate=False), a_ln_g, a_ln_b)
    y_a = u * spatial_gating(v, a_ws, a_bias) * jax.nn.silu(ga)
    a = bv * jax.nn.sigmoid(bg)
    c, new_hist = depthwise_causal_conv(conv_hist, a, b_conv_w, b_conv_b)
    y_b = jax.nn.silu(layer_norm(c, b_ln_g, b_ln_b)) * jax.nn.silu(gb)
    y = jnp.einsum('bse,ed->bsd', jnp.concatenate([y_a, y_b], axis=-1), w_out)
    x_new = layer_norm(ALPHA * x + y, post_ln_g, post_ln_b)
    return x_new, new_hist, v


def setup_inputs(seed: int = 0) -> dict:
    key = jax.random.key(seed)
    ks = jax.random.split(key, 16)
    f32 = jnp.float32
    nrm = lambda k, s: jax.random.normal(k, s, dtype=f32)
    return {
        "x_prompt": nrm(ks[0], (BATCH, SEQ, D_MODEL)),
        "x_sample": nrm(ks[1], (DEC_BATCH, DEC_SEQ, D_MODEL)),
        "cache_conv": 0.5 * nrm(ks[2], (DEPTH, DEC_BATCH, CONV_STATE, W_B)),
        "w_in": nrm(ks[3], (DEPTH, D_MODEL, D_IN)) * D_MODEL ** -0.5,
        "a_ln_g": 1.0 + 0.02 * nrm(ks[4], (DEPTH, W_A)),
        "a_ln_b": 0.02 * nrm(ks[5], (DEPTH, W_A)),
        "a_ws": nrm(ks[6], (DEPTH, H_A, A_CHUNK, A_CHUNK)) * (0.5 * A_CHUNK ** -0.5),
        "a_bias": 1.0 + 0.02 * nrm(ks[7], (DEPTH, H_A, A_CHUNK)),
        "b_conv_w": nrm(ks[8], (DEPTH, CONV_K, W_B)) * CONV_K ** -0.5,
        "b_conv_b": 0.02 * nrm(ks[9], (DEPTH, W_B)),
        "b_ln_g": 1.0 + 0.02 * nrm(ks[10], (DEPTH, W_B)),
        "b_ln_b": 0.02 * nrm(ks[11], (DEPTH, W_B)),
        "w_out": nrm(ks[12], (DEPTH, D_MIX, D_MODEL)) * (D_MIX ** -0.5 * BETA),
        "post_ln_g": 1.0 + 0.02 * nrm(ks[13], (DEPTH, D_MODEL)),
        "post_ln_b": 0.02 * nrm(ks[14], (DEPTH, D_MODEL)),
    }


def reference(x_prompt, x_sample, cache_conv, w_in, a_ln_g, a_ln_b, a_ws, a_bias,
              b_conv_w, b_conv_b, b_ln_g, b_ln_b, w_out, post_ln_g, post_ln_b):
    xp = x_prompt
    xs = x_sample
    zero_hist = jnp.zeros((x_prompt.shape[0], CONV_STATE, W_B), dtype=x_prompt.dtype)
    conv_p, conv_s, av_s = [], [], []
    for l in range(DEPTH):
        params = (w_in[l], a_ln_g[l], a_ln_b[l], a_ws[l], a_bias[l], b_conv_w[l], b_conv_b[l],
                  b_ln_g[l], b_ln_b[l], w_out[l], post_ln_g[l], post_ln_b[l])
        xp, hist_p, _ = layer(xp, zero_hist, *params)
        xs, hist_s, v_s = layer(xs, cache_conv[l].astype(xs.dtype), *params)
        conv_p.append(hist_p)
        conv_s.append(hist_s)
        av_s.append(v_s)
    new_conv_prompt = jnp.stack(conv_p)
    new_conv_sample = jnp.stack(conv_s)
    new_av_sample = jnp.stack(av_s)
    return (xp, xs, new_conv_prompt, new_conv_sample, new_av_sample)
```

```python
import functools
import math

import jax
import jax.numpy as jnp
from jax import lax
from jax.experimental import pallas as pl
from jax.experimental.pallas import tpu as pltpu

F32 = jnp.float32
BF16 = jnp.bfloat16

LANES = 128
LN_EPS = 1e-5
N_HEADS = 8
A_CHUNK = 128
CONV_K = 31
CONV_STATE = CONV_K - 1
HIST_PAD = 32
PROMPT_TILE = 512
VMEM_LIMIT_BYTES = 56 * 1024 * 1024


def _layer_norm(x, g, b):
    mu = jnp.mean(x, axis=-1, keepdims=True)
    xc = x - mu
    var = jnp.mean(xc * xc, axis=-1, keepdims=True)
    return xc * lax.rsqrt(var + LN_EPS) * g + b


def _gelu(x):
    return 0.5 * x * (1.0 + lax.erf(x * (1.0 / math.sqrt(2.0))))


def _silu(x):
    return x * jax.nn.sigmoid(x)


def _layer_kernel(*refs, ns, lt, lc, alpha, has_hist, want_av):
    it = iter(refs)
    x_ref = next(it)
    hist_ref = next(it) if has_hist else None
    (w_in_ref, w_out_ref, wcat_ref, bias_ref, a_g_ref, a_b_ref, cw_ref, cb_ref,
     b_g_ref, b_b_ref, p_g_ref, p_b_ref) = (next(it) for _ in range(12))
    out_ref = next(it)
    conv_ref = next(it)
    av_ref = next(it) if want_av else None
    abuf_ref = next(it)
    wm_ref = next(it)

    wa = a_g_ref.shape[-1]
    wb = b_g_ref.shape[-1]
    n_sub = lt // lc
    rb = ns * lc
    t = pl.program_id(1)

    @pl.when(t == 0)
    def _():
        if has_hist:
            abuf_ref[:, HIST_PAD - CONV_STATE:HIST_PAD, :] = hist_ref[...]
        else:
            abuf_ref[:, 0:HIST_PAD, :] = jnp.zeros((ns, HIST_PAD, wb), F32)

    @pl.when((pl.program_id(0) == 0) & (t == 0))
    def _():
        q = lax.broadcasted_iota(jnp.int32, (lc, 2 * lc), 0)
        k = lax.broadcasted_iota(jnp.int32, (lc, 2 * lc), 1)
        k = jnp.where(k >= lc, k - lc, k)
        for j in range(wm_ref.shape[0]):
            wm_ref[j] = jnp.where(k <= q, wcat_ref[j], jnp.zeros((), BF16))

    left = lax.broadcasted_iota(jnp.int32, (rb, LANES), 1) < (LANES // 2)
    o_u, o_v, o_ga, o_bv, o_bg, o_gb = 0, wa, 2 * wa, 3 * wa, 3 * wa + wb, 3 * wa + 2 * wb

    for c in range(n_sub):
        r0 = c * rb
        xc = x_ref[r0:r0 + rb, :]
        xb = xc.astype(BF16)

        def proj(off, width):
            return jnp.dot(xb, w_in_ref[:, off:off + width], preferred_element_type=F32)

        u = _gelu(proj(o_u, wa))
        v = _layer_norm(_gelu(proj(o_v, wa)), a_g_ref[...], a_b_ref[...])
        if want_av:
            av_ref[r0:r0 + rb, :] = v
        vb = v.astype(BF16)
        zero = jnp.zeros((), BF16)
        cols = []
        for j in range(wa // LANES):
            vj = vb[:, j * LANES:(j + 1) * LANES]
            vl = jnp.where(left, vj, zero)
            vr = jnp.where(left, zero, vj)
            outs = []
            for s in range(ns):
                rhs = jnp.concatenate([vl[s * lc:(s + 1) * lc], vr[s * lc:(s + 1) * lc]], axis=0)
                outs.append(jnp.dot(wm_ref[j], rhs, preferred_element_type=F32))
            cols.append(outs[0] if ns == 1 else jnp.concatenate(outs, axis=0))
        sg = jnp.concatenate(cols, axis=1)
        sg = (sg.reshape(ns, lc, wa) + bias_ref[...][None]).reshape(rb, wa)
        y_a = u * sg * _silu(proj(o_ga, wa))

        a = proj(o_bv, wb) * jax.nn.sigmoid(proj(o_bg, wb))
        a0 = HIST_PAD + c * lc
        abuf_ref[:, a0:a0 + lc, :] = a.reshape(ns, lc, wb)
        base = a0 - CONV_STATE
        acc = cb_ref[...][None] + cw_ref[0:1, :][None] * abuf_ref[:, base:base + lc, :]
        for k in range(1, CONV_K):
            acc = acc + cw_ref[k:k + 1, :][None] * abuf_ref[:, base + k:base + k + lc, :]
        cn = _layer_norm(acc.reshape(rb, wb), b_g_ref[...], b_b_ref[...])
        y_b = _silu(cn) * _silu(proj(o_gb, wb))

        cat = jnp.concatenate([y_a, y_b], axis=-1).astype(BF16)
        y = jnp.dot(cat, w_out_ref[...], preferred_element_type=F32)
        out_ref[r0:r0 + rb, :] = _layer_norm(alpha * xc + y, p_g_ref[...], p_b_ref[...])

    @pl.when(t == pl.num_programs(1) - 1)
    def _():
        conv_ref[...] = abuf_ref[:, HIST_PAD + lt - CONV_STATE:HIST_PAD + lt, :]

    if n_sub * lc >= HIST_PAD:
        @pl.when(t < pl.num_programs(1) - 1)
        def _():
            abuf_ref[:, 0:HIST_PAD, :] = abuf_ref[:, lt:lt + HIST_PAD, :]


def _run_layer(x, hist, params, *, ns, lt, lc, alpha, want_av):
    (w_in, w_out, wcat, bias_full, a_g, a_b, cw, cb, b_g, b_b, p_g, p_b) = params
    g_count, t_total, d = x.shape
    rows = ns * lt
    n_tiles = t_total // rows
    assert n_tiles * rows == t_total and lt % lc == 0
    assert ns == 1 or lt == lc
    assert hist is None or n_tiles == 1
    wa, wb = a_g.shape[-1], b_g.shape[-1]

    def const_spec(arr):
        nd = arr.ndim
        return pl.BlockSpec(arr.shape, lambda g, t, _nd=nd: (0,) * _nd)

    in_specs = [pl.BlockSpec((None, rows, d), lambda g, t: (g, t, 0))]
    args = [x]
    if hist is not None:
        in_specs.append(pl.BlockSpec((ns, CONV_STATE, wb), lambda g, t: (g, 0, 0)))
        args.append(hist)
    for arr in params:
        in_specs.append(const_spec(arr))
        args.append(arr)

    out_shape = [jax.ShapeDtypeStruct(x.shape, F32),
                 jax.ShapeDtypeStruct((g_count * ns, CONV_STATE, wb), F32)]
    out_specs = [pl.BlockSpec((None, rows, d), lambda g, t: (g, t, 0)),
                 pl.BlockSpec((ns, CONV_STATE, wb), lambda g, t: (g, 0, 0))]
    if want_av:
        out_shape.append(jax.ShapeDtypeStruct((g_count, t_total, wa), F32))
        out_specs.append(pl.BlockSpec((None, rows, wa), lambda g, t: (g, t, 0)))

    body = functools.partial(_layer_kernel, ns=ns, lt=lt, lc=lc, alpha=alpha,
                             has_hist=hist is not None, want_av=want_av)
    return pl.pallas_call(
        body,
        grid=(g_count, n_tiles),
        in_specs=in_specs,
        out_specs=out_specs,
        out_shape=out_shape,
        scratch_shapes=[pltpu.VMEM((ns, HIST_PAD + lt, wb), F32),
                        pltpu.VMEM((wcat.shape[0], lc, 2 * lc), BF16)],
        compiler_params=pltpu.CompilerParams(
            dimension_semantics=("arbitrary", "arbitrary"),
            vmem_limit_bytes=VMEM_LIMIT_BYTES),
    )(*args)


def _spatial_params(ws, bias, lc):
    h = ws.shape[0]
    w = ws[:, :lc, :lc].reshape(h // 2, 2, lc, lc)
    wcat = jnp.transpose(w, (0, 2, 1, 3)).reshape(h // 2, lc, 2 * lc).astype(BF16)
    dh = LANES // 2
    bias_full = jnp.repeat(jnp.transpose(bias[:, :lc]), dh, axis=1)
    return wcat, bias_full


def kernel(x_prompt, x_sample, cache_conv, w_in, a_ln_g, a_ln_b, a_ws, a_bias, b_conv_w, b_conv_b,
           b_ln_g, b_ln_b, w_out, post_ln_g, post_ln_b):
    depth = w_in.shape[0]
    alpha = (2 * depth) ** 0.25
    dec_batch, dec_seq, d = x_sample.shape
    lc_s = min(dec_seq, A_CHUNK)

    xp = x_prompt
    xs = x_sample.reshape(1, dec_batch * dec_seq, d)
    conv_p, conv_s, av_s = [], [], []
    row = lambda vec: vec.reshape(1, -1)
    for l in range(depth):
        common = (w_in[l].astype(BF16), w_out[l].astype(BF16))
        tail = (row(a_ln_g[l]), row(a_ln_b[l]), b_conv_w[l], row(b_conv_b[l]),
                row(b_ln_g[l]), row(b_ln_b[l]), row(post_ln_g[l]), row(post_ln_b[l]))
        sp_p = _spatial_params(a_ws[l], a_bias[l], A_CHUNK)
        sp_s = _spatial_params(a_ws[l], a_bias[l], lc_s)

        xp, hist_p = _run_layer(xp, None, common + sp_p + tail, ns=1, lt=PROMPT_TILE, lc=A_CHUNK,
                                alpha=alpha, want_av=False)
        xs, hist_s, v_s = _run_layer(xs, cache_conv[l], common + sp_s + tail, ns=dec_batch,
                                     lt=dec_seq, lc=lc_s, alpha=alpha, want_av=True)
        conv_p.append(hist_p)
        conv_s.append(hist_s)
        av_s.append(v_s.reshape(dec_batch, dec_seq, -1))
    return (xp, xs.reshape(x_sample.shape), jnp.stack(conv_p), jnp.stack(conv_s), jnp.stack(av_s))
```

```python
import functools
import math

import jax
import jax.numpy as jnp
from jax import lax
from jax.experimental import pallas as pl
from jax.experimental.pallas import tpu as pltpu

F32 = jnp.float32
BF16 = jnp.bfloat16

LANES = 128
SUBLANES = 8
CONV_GROUP = 8
W_COPY_ROWS = 64
LN_EPS = 1e-5
A_CHUNK = 128
CONV_K = 31
CONV_STATE = CONV_K - 1
HIST_PAD = 32
PROMPT_TILE = 512
VMEM_LIMIT_BYTES = 56 * 1024 * 1024


def _layer_norm(x, g, b):
    mu = jnp.mean(x, axis=-1, keepdims=True)
    xc = x - mu
    var = jnp.mean(xc * xc, axis=-1, keepdims=True)
    return xc * lax.rsqrt(var + LN_EPS) * g + b


def _gelu(x):
    return 0.5 * x * (1.0 + lax.erf(x * (1.0 / math.sqrt(2.0))))


def _silu(x):
    return x * jax.nn.sigmoid(x)


def _layer_kernel(*refs, ns, lt, lc, alpha, has_hist, want_av):
    it = iter(refs)
    x_ref = next(it)
    hist_ref = next(it) if has_hist else None
    (w_in_hbm_ref, w_out_hbm_ref, wcat_ref, bias_ref, a_g_ref, a_b_ref, cw_ref, cb_ref,
     b_g_ref, b_b_ref, p_g_ref, p_b_ref) = (next(it) for _ in range(12))
    out_ref = next(it)
    conv_ref = next(it)
    av_ref = next(it) if want_av else None
    abuf_ref = next(it)
    wm_ref = next(it)
    w_in_ref = next(it)
    w_out_ref = next(it)

    wa = a_g_ref.shape[-1]
    wb = b_g_ref.shape[-1]
    n_lt = wb // LANES
    n_sub = lt // lc
    rb = ns * lc
    t = pl.program_id(1)

    @pl.when(t == 0)
    def _():
        for j in range(n_lt):
            if has_hist:
                abuf_ref[j, :, HIST_PAD - CONV_STATE:HIST_PAD, :] = hist_ref[:, :, j * LANES:(j + 1) * LANES]
            else:
                abuf_ref[j, :, 0:HIST_PAD, :] = jnp.zeros((ns, HIST_PAD, LANES), F32)

    @pl.when((pl.program_id(0) == 0) & (t == 0))
    def _():
        def copy_rows(i, carry):
            rows = pl.ds(pl.multiple_of(i * W_COPY_ROWS, W_COPY_ROWS), W_COPY_ROWS)
            w_in_ref[rows, :] = w_in_hbm_ref[rows, :]
            w_out_ref[rows, :] = w_out_hbm_ref[rows, :]
            return carry
        lax.fori_loop(0, w_in_ref.shape[0] // W_COPY_ROWS, copy_rows, 0)
        q = lax.broadcasted_iota(jnp.int32, (lc, 2 * lc), 0)
        k = lax.broadcasted_iota(jnp.int32, (lc, 2 * lc), 1)
        k = jnp.where(k >= lc, k - lc, k)
        for j in range(wm_ref.shape[0]):
            wm_ref[j] = jnp.where(k <= q, wcat_ref[j], jnp.zeros((), BF16))

    left = lax.broadcasted_iota(jnp.int32, (rb, LANES), 1) < (LANES // 2)
    o_u, o_v, o_ga, o_bv, o_bg, o_gb = 0, wa, 2 * wa, 3 * wa, 3 * wa + wb, 3 * wa + 2 * wb

    def project(c, h):
        xb = x_ref[c * rb:(c + 1) * rb, :].astype(BF16)
        for off, width in ((o_u, wa), (o_v, wa), (o_ga, wa), (o_bv, wb), (o_bg, wb), (o_gb, wb)):
            h.append(jnp.dot(xb, w_in_ref[:, off:off + width], preferred_element_type=F32))
            yield

    def mix(c, h):
        h_u, h_v, h_ga, h_bv, h_bg, h_gb = h
        r0 = c * rb
        u = _gelu(h_u)
        v = _layer_norm(_gelu(h_v), a_g_ref[...], a_b_ref[...])
        yield
        if want_av:
            av_ref[r0:r0 + rb, :] = v
        vb = v.astype(BF16)
        zero = jnp.zeros((), BF16)
        cols = []
        for j in range(wa // LANES):
            vj = vb[:, j * LANES:(j + 1) * LANES]
            vl = jnp.where(left, vj, zero)
            vr = jnp.where(left, zero, vj)
            outs = []
            for s in range(ns):
                rhs = jnp.concatenate([vl[s * lc:(s + 1) * lc], vr[s * lc:(s + 1) * lc]], axis=0)
                outs.append(jnp.dot(wm_ref[j], rhs, preferred_element_type=F32))
            cols.append(outs[0] if ns == 1 else jnp.concatenate(outs, axis=0))
        sg = jnp.concatenate(cols, axis=1)
        sg = (sg.reshape(ns, lc, wa) + bias_ref[...][None]).reshape(rb, wa)
        y_a = u * sg * _silu(h_ga)

        a = h_bv * jax.nn.sigmoid(h_bg)
        a0 = HIST_PAD + c * lc
        base = a0 - CONV_STATE
        for j in range(n_lt):
            abuf_ref[j, :, a0:a0 + lc, :] = a[:, j * LANES:(j + 1) * LANES].reshape(ns, lc, LANES)
        yield
        conv_cols = []
        for j in range(n_lt):
            ls = slice(j * LANES, (j + 1) * LANES)
            taps = [jnp.broadcast_to(cw_ref[k:k + 1, ls], (SUBLANES, LANES)) for k in range(CONV_K)]
            bias_blk = jnp.broadcast_to(cb_ref[:, ls], (SUBLANES, LANES))
            blocks = []
            for s in range(ns):
                for g0 in range(0, lc, SUBLANES * CONV_GROUP):
                    n_blk = min(CONV_GROUP, (lc - g0) // SUBLANES)
                    accs = [bias_blk] * n_blk
                    for o in range(SUBLANES * (n_blk - 1) + CONV_K):
                        r = base + g0 + o
                        window = abuf_ref[j, s, r:r + SUBLANES, :]
                        for blk in range(n_blk):
                            k = o - SUBLANES * blk
                            if 0 <= k < CONV_K:
                                accs[blk] = accs[blk] + taps[k] * window
                    blocks.extend(accs)
            conv_cols.append(jnp.concatenate(blocks, axis=0))
            yield
        cn = _layer_norm(jnp.concatenate(conv_cols, axis=1), b_g_ref[...], b_b_ref[...])
        y_b = _silu(cn) * _silu(h_gb)

        cat = jnp.concatenate([y_a, y_b], axis=-1).astype(BF16)
        y = jnp.dot(cat, w_out_ref[...], preferred_element_type=F32)
        out_ref[r0:r0 + rb, :] = _layer_norm(alpha * x_ref[r0:r0 + rb, :] + y,
                                             p_g_ref[...], p_b_ref[...])

    h_cur = []
    for _ in project(0, h_cur):
        pass
    for c in range(n_sub):
        h_next = []
        stages = [mix(c, h_cur)]
        if c + 1 < n_sub:
            stages.insert(0, project(c + 1, h_next))
        while stages:
            for gen in list(stages):
                if next(gen, stages) is stages:
                    stages.remove(gen)
        h_cur = h_next

    @pl.when(t == pl.num_programs(1) - 1)
    def _():
        for j in range(n_lt):
            conv_ref[:, :, j * LANES:(j + 1) * LANES] = (
                abuf_ref[j, :, HIST_PAD + lt - CONV_STATE:HIST_PAD + lt, :])

    if lt >= HIST_PAD:
        @pl.when(t < pl.num_programs(1) - 1)
        def _():
            abuf_ref[:, :, 0:HIST_PAD, :] = abuf_ref[:, :, lt:lt + HIST_PAD, :]


def _run_layer(x, hist, params, *, ns, lt, lc, alpha, want_av):
    (w_in, w_out, wcat, bias_full, a_g, a_b, cw, cb, b_g, b_b, p_g, p_b) = params
    g_count, t_total, d = x.shape
    rows = ns * lt
    n_tiles = t_total // rows
    assert n_tiles * rows == t_total and lt % lc == 0
    assert ns == 1 or lt == lc
    assert hist is None or n_tiles == 1
    wa, wb = a_g.shape[-1], b_g.shape[-1]

    def const_spec(arr):
        nd = arr.ndim
        return pl.BlockSpec(arr.shape, lambda g, t, _nd=nd: (0,) * _nd,
                            pipeline_mode=pl.Buffered(1))

    in_specs = [pl.BlockSpec((None, rows, d), lambda g, t: (g, t, 0))]
    args = [x]
    if hist is not None:
        in_specs.append(pl.BlockSpec((ns, CONV_STATE, wb), lambda g, t: (g, 0, 0)))
        args.append(hist)
    for arr in params:
        in_specs.append(const_spec(arr))
        args.append(arr)

    out_shape = [jax.ShapeDtypeStruct(x.shape, F32),
                 jax.ShapeDtypeStruct((g_count * ns, CONV_STATE, wb), F32)]
    out_specs = [pl.BlockSpec((None, rows, d), lambda g, t: (g, t, 0)),
                 pl.BlockSpec((ns, CONV_STATE, wb), lambda g, t: (g, 0, 0))]
    if want_av:
        out_shape.append(jax.ShapeDtypeStruct((g_count, t_total, wa), F32))
        out_specs.append(pl.BlockSpec((None, rows, wa), lambda g, t: (g, t, 0)))

    body = functools.partial(_layer_kernel, ns=ns, lt=lt, lc=lc, alpha=alpha,
                             has_hist=hist is not None, want_av=want_av)
    return pl.pallas_call(
        body,
        grid=(g_count, n_tiles),
        in_specs=in_specs,
        out_specs=out_specs,
        out_shape=out_shape,
        scratch_shapes=[pltpu.VMEM((wb // LANES, ns, HIST_PAD + lt, LANES), F32),
                        pltpu.VMEM((wcat.shape[0], lc, 2 * lc), BF16),
                        pltpu.VMEM(w_in.shape, BF16),
                        pltpu.VMEM(w_out.shape, BF16)],
        compiler_params=pltpu.CompilerParams(
            dimension_semantics=("arbitrary", "arbitrary"),
            vmem_limit_bytes=VMEM_LIMIT_BYTES),
    )(*args)


def _spatial_params(ws, bias, lc):
    h = ws.shape[0]
    w = ws[:, :lc, :lc].reshape(h // 2, 2, lc, lc)
    wcat = jnp.transpose(w, (0, 2, 1, 3)).reshape(h // 2, lc, 2 * lc).astype(BF16)
    dh = LANES // 2
    bias_full = jnp.repeat(jnp.transpose(bias[:, :lc]), dh, axis=1)
    return wcat, bias_full


def kernel(x_prompt, x_sample, cache_conv, w_in, a_ln_g, a_ln_b, a_ws, a_bias, b_conv_w, b_conv_b,
           b_ln_g, b_ln_b, w_out, post_ln_g, post_ln_b):
    depth = w_in.shape[0]
    alpha = (2 * depth) ** 0.25
    dec_batch, dec_seq, d = x_sample.shape
    lc_s = min(dec_seq, A_CHUNK)

    xp = x_prompt
    xs = x_sample.reshape(1, dec_batch * dec_seq, d)
    conv_p, conv_s, av_s = [], [], []
    row = lambda vec: vec.reshape(1, -1)
    for l in range(depth):
        common = (w_in[l].astype(BF16), w_out[l].astype(BF16))
        tail = (row(a_ln_g[l]), row(a_ln_b[l]), b_conv_w[l], row(b_conv_b[l]),
                row(b_ln_g[l]), row(b_ln_b[l]), row(post_ln_g[l]), row(post_ln_b[l]))
        sp_p = _spatial_params(a_ws[l], a_bias[l], A_CHUNK)
        sp_s = _spatial_params(a_ws[l], a_bias[l], lc_s)

        xp, hist_p = _run_layer(xp, None, common + sp_p + tail, ns=1, lt=PROMPT_TILE, lc=A_CHUNK,
                                alpha=alpha, want_av=False)
        xs, hist_s, v_s = _run_layer(xs, cache_conv[l], common + sp_s + tail, ns=dec_batch,
                                     lt=dec_seq, lc=lc_s, alpha=alpha, want_av=True)
        conv_p.append(hist_p)
        conv_s.append(hist_s)
        av_s.append(v_s.reshape(dec_batch, dec_seq, -1))
    return (xp, xs.reshape(x_sample.shape), jnp.stack(conv_p), jnp.stack(conv_s), jnp.stack(av_s))
```
